```python
import math
import jax, jax.numpy as jnp
from jax import lax
import numpy as np

D_MODEL = 1024
BATCH = 8
SEQ = 4096
DEPTH = 2
DEC_BATCH = 32
DEC_SEQ = 1
PAST_LEN = 16384
PAGE_SIZE = 128

D_MIX = D_MODEL
D_SSD = D_MIX // 2
SSD_HEAD_DIM = 64
SSD_HEADS = D_SSD // SSD_HEAD_DIM
SSD_GROUPS = 2
SSD_STATE = 64
SSD_CONV_W = 4
SSD_CHUNK = 128
CONV_DIM = D_SSD + 2 * SSD_GROUPS * SSD_STATE
D_SB = D_MIX - D_SSD
SB_HEAD_DIM = 64
SB_HEADS = D_SB // SB_HEAD_DIM
SB_BLOCK = 128
SB_BIAS_INIT = -7.5
D_FF = 2816
D_IN_PROJ = D_SSD + CONV_DIM + SSD_HEADS + 3 * D_SB
N_NORMS = 6
RMS_EPS = 1e-6

kernel_name = "hymba_ssd_stickbreaking_macaron_step"


def rmsnorm(x, g):
    xf = x.astype(jnp.float32)
    xf = xf * lax.rsqrt(jnp.mean(xf * xf, axis=-1, keepdims=True) + RMS_EPS)
    return (xf * g.astype(jnp.float32)).astype(x.dtype)


def swiglu(h, w_up, w_down):
    gate, up = jnp.split(h @ w_up, 2, axis=-1)
    return (jax.nn.silu(gate) * up) @ w_down


def causal_conv(xbc, prev, w, b):
    L = xbc.shape[1]
    full = jnp.concatenate([prev.astype(xbc.dtype), xbc], axis=1)
    out = b + sum(w[j] * full[:, j:j + L] for j in range(SSD_CONV_W))
    return jax.nn.silu(out), full[:, L:]


def ssd_scan(x, dt, a, bm, cm, h0):
    bsz, L = x.shape[:2]
    f32 = jnp.float32
    R = SSD_HEADS // SSD_GROUPS
    lc = min(SSD_CHUNK, L)
    pad = (-L) % lc
    if pad:
        x = jnp.pad(x, ((0, 0), (0, pad), (0, 0), (0, 0)))
        dt = jnp.pad(dt, ((0, 0), (0, pad), (0, 0)))
        bm = jnp.pad(bm, ((0, 0), (0, pad), (0, 0), (0, 0)))
        cm = jnp.pad(cm, ((0, 0), (0, pad), (0, 0), (0, 0)))
    nc = (L + pad) // lc
    xc = x.reshape(bsz, nc, lc, SSD_GROUPS, R, SSD_HEAD_DIM).astype(f32)
    dtc = dt.reshape(bsz, nc, lc, SSD_GROUPS, R).astype(f32)
    bc = bm.reshape(bsz, nc, lc, SSD_GROUPS, SSD_STATE).astype(f32)
    cc = cm.reshape(bsz, nc, lc, SSD_GROUPS, SSD_STATE).astype(f32)
    acum = jnp.cumsum(dtc * a.astype(f32).reshape(SSD_GROUPS, R), axis=2)
    causal = jnp.tril(jnp.ones((lc, lc), dtype=bool))[None, None, :, :, None, None]
    seg = acum[:, :, :, None] - acum[:, :, None, :]
    decay = jnp.exp(jnp.where(causal, seg, -jnp.inf))
    cb = jnp.einsum('bclgn,bcsgn->bclsg', cc, bc)
    y_diag = jnp.einsum('bclsg,bclsgr,bcsgr,bcsgrp->bclgrp', cb, decay, dtc, xc)
    decay_end = jnp.exp(acum[:, :, -1:] - acum)
    chunk_states = jnp.einsum('bcsgn,bcsgr,bcsgrp->bcgrpn', bc, decay_end * dtc, xc)
    chunk_decay = jnp.exp(acum[:, :, -1])

    def step(h, inp):
        s_c, d_c = inp
        return d_c[..., None, None] * h + s_c, h

    h0g = h0.astype(f32).reshape(bsz, SSD_GROUPS, R, SSD_HEAD_DIM, SSD_STATE)
    h_final, h_start = lax.scan(step, h0g, (jnp.moveaxis(chunk_states, 1, 0), jnp.moveaxis(chunk_decay, 1, 0)))
    h_start = jnp.moveaxis(h_start, 0, 1)
    y_off = jnp.einsum('bclgn,bcgrpn,bclgr->bclgrp', cc, h_start, jnp.exp(acum))
    y = (y_diag + y_off).reshape(bsz, nc * lc, SSD_HEADS, SSD_HEAD_DIM)[:, :L]
    return y, h_final.reshape(bsz, SSD_HEADS, SSD_HEAD_DIM, SSD_STATE)


def stick_breaking_block(q, k, v, q_pos, k_pos, sb_bias):
    z = jnp.einsum('bqhd,bkhd->bhqk', q.astype(jnp.float32), k.astype(jnp.float32)) * (SB_HEAD_DIM ** -0.5)
    z = z + sb_bias.astype(jnp.float32)[None, :, None, None]
    mask = (k_pos[None, :] < q_pos[:, None])[None, None]
    log_keep = jnp.where(mask, jax.nn.log_sigmoid(-z), 0.0)
    after = lax.cumsum(log_keep, axis=3, reverse=True) - log_keep
    attn = jnp.where(mask, jnp.exp(jax.nn.log_sigmoid(z) + after), 0.0)
    return jnp.einsum('bhqk,bkhd->bqhd', attn, v.astype(jnp.float32)).astype(q.dtype)


def stick_breaking_attention(q, k, v, q_offset, sb_bias):
    tq = q.shape[1]
    outs = []
    for start in range(0, tq, SB_BLOCK):
        end = min(start + SB_BLOCK, tq)
        kend = q_offset + end
        outs.append(stick_breaking_block(q[:, start:end], k[:, :kend], v[:, :kend],
                                         q_offset + jnp.arange(start, end), jnp.arange(kend), sb_bias))
    return jnp.concatenate(outs, axis=1)


def mixer(h, prev_conv, h0, past_k, past_v, q_offset, w_in, conv_w, conv_b, dt_bias, a_log,
          d_skip, ssd_norm_g, sb_norm_g, sb_bias, w_out):
    bsz, L, _ = h.shape
    u = h @ w_in
    o1 = D_SSD
    o2 = o1 + CONV_DIM
    o3 = o2 + SSD_HEADS
    o4 = o3 + D_SB
    o5 = o4 + D_SB
    z, xbc, dt_raw = u[..., :o1], u[..., o1:o2], u[..., o2:o3]
    q, k, v = u[..., o3:o4], u[..., o4:o5], u[..., o5:]
    xbc, new_conv = causal_conv(xbc, prev_conv, conv_w, conv_b)
    xs = xbc[..., :D_SSD].reshape(bsz, L, SSD_HEADS, SSD_HEAD_DIM)
    bm = xbc[..., D_SSD:D_SSD + SSD_GROUPS * SSD_STATE].reshape(bsz, L, SSD_GROUPS, SSD_STATE)
    cm = xbc[..., D_SSD + SSD_GROUPS * SSD_STATE:].reshape(bsz, L, SSD_GROUPS, SSD_STATE)
    dt = jax.nn.softplus(dt_raw.astype(jnp.float32) + dt_bias.astype(jnp.float32))
    a = -jnp.exp(a_log.astype(jnp.float32))
    y, h_final = ssd_scan(xs, dt, a, bm, cm, h0)
    y = y + d_skip.astype(jnp.float32)[:, None] * xs.astype(jnp.float32)
    y = y.reshape(bsz, L, D_SSD).astype(h.dtype)
    y_ssd = rmsnorm(y * jax.nn.silu(z), ssd_norm_g)
    kh = k.reshape(bsz, L, SB_HEADS, SB_HEAD_DIM)
    vh = v.reshape(bsz, L, SB_HEADS, SB_HEAD_DIM)
    qh = q.reshape(bsz, L, SB_HEADS, SB_HEAD_DIM)
    if past_k is None:
        k_all, v_all = kh, vh
    else:
        k_all = jnp.concatenate([past_k.astype(kh.dtype), kh], axis=1)
        v_all = jnp.concatenate([past_v.astype(vh.dtype), vh], axis=1)
    o = stick_breaking_attention(qh, k_all, v_all, q_offset, sb_bias)
    y_sb = rmsnorm(o.reshape(bsz, L, D_SB), sb_norm_g)
    out = jnp.concatenate([y_ssd, y_sb], axis=-1) @ w_out
    return out, new_conv, h_final, kh, vh


def layer_forward(x, prev_conv, h0, past_k, past_v, q_offset, norm_g, ffn1_up, ffn1_down, w_in,
                  conv_w, conv_b, dt_bias, a_log, d_skip, ssd_norm_g, sb_norm_g, sb_bias, w_out,
                  ffn2_up, ffn2_down):
    x = x + 0.5 * rmsnorm(swiglu(rmsnorm(x, norm_g[0]), ffn1_up, ffn1_down), norm_g[1])
    m, new_conv, h_final, kh, vh = mixer(rmsnorm(x, norm_g[2]), prev_conv, h0, past_k, past_v, q_offset,
                                         w_in, conv_w, conv_b, dt_bias, a_log, d_skip,
                                         ssd_norm_g, sb_norm_g, sb_bias, w_out)
    x = x + rmsnorm(m, norm_g[3])
    x = x + 0.5 * rmsnorm(swiglu(rmsnorm(x, norm_g[4]), ffn2_up, ffn2_down), norm_g[5])
    return x, new_conv, h_final, kh, vh


def setup_inputs(seed: int = 0) -> dict:
    key = jax.random.key(seed)
    ks = jax.random.split(key, 24)
    f32 = jnp.float32

    def nrm(k, shape, scale):
        return jax.random.normal(k, shape, f32) * scale

    n_pages = PAST_LEN // PAGE_SIZE
    n_used = DEC_BATCH * n_pages
    n_pool = n_used + n_used // 4
    page_table = jax.random.permutation(ks[0], n_pool)[:n_used].reshape(DEC_BATCH, n_pages).astype(jnp.int32)
    dt_init = jnp.exp(jax.random.uniform(ks[1], (DEPTH, SSD_HEADS), f32, math.log(1e-3), math.log(1e-1)))
    dt_bias = dt_init + jnp.log(-jnp.expm1(-dt_init))
    a_log = jnp.log(jax.random.uniform(ks[2], (DEPTH, SSD_HEADS), f32, 1.0, 16.0))
    return {
        "x_prompt": nrm(ks[3], (BATCH, SEQ, D_MODEL), 1.0),
        "x_sample": nrm(ks[4], (DEC_BATCH, DEC_SEQ, D_MODEL), 1.0),
        "cache_k": nrm(ks[5], (DEPTH, n_pool, PAGE_SIZE, SB_HEADS, SB_HEAD_DIM), 1.0),
        "cache_v": nrm(ks[6], (DEPTH, n_pool, PAGE_SIZE, SB_HEADS, SB_HEAD_DIM), 1.0),
        "state_ssm": nrm(ks[7], (DEPTH, DEC_BATCH, SSD_HEADS, SSD_HEAD_DIM, SSD_STATE), 0.3),
        "state_conv": nrm(ks[8], (DEPTH, DEC_BATCH, SSD_CONV_W - 1, CONV_DIM), 1.0),
        "page_table": page_table,
        "norm_g": 1.0 + nrm(ks[9], (DEPTH, N_NORMS, D_MODEL), 0.02),
        "ffn1_up": nrm(ks[10], (DEPTH, D_MODEL, 2 * D_FF), D_MODEL ** -0.5),
        "ffn1_down": nrm(ks[11], (DEPTH, D_FF, D_MODEL), D_FF ** -0.5),
        "w_in": nrm(ks[12], (DEPTH, D_MODEL, D_IN_PROJ), D_MODEL ** -0.5),
        "conv_w": nrm(ks[13], (DEPTH, SSD_CONV_W, CONV_DIM), SSD_CONV_W ** -0.5),
        "conv_b": nrm(ks[14], (DEPTH, CONV_DIM), 0.02),
        "dt_bias": dt_bias,
        "a_log": a_log,
        "d_skip": 1.0 + nrm(ks[15], (DEPTH, SSD_HEADS), 0.1),
        "ssd_norm_g": 1.0 + nrm(ks[16], (DEPTH, D_SSD), 0.02),
        "sb_norm_g": 1.0 + nrm(ks[17], (DEPTH, D_SB), 0.02),
        "sb_bias": SB_BIAS_INIT + nrm(ks[21], (DEPTH, SB_HEADS), 0.3),
        "w_out": nrm(ks[18], (DEPTH, D_MIX, D_MODEL), D_MIX ** -0.5),
        "ffn2_up": nrm(ks[19], (DEPTH, D_MODEL, 2 * D_FF), D_MODEL ** -0.5),
        "ffn2_down": nrm(ks[20], (DEPTH, D_FF, D_MODEL), D_FF ** -0.5),
    }


def reference(x_prompt, x_sample, cache_k, cache_v, state_ssm, state_conv, page_table,
              norm_g, ffn1_up, ffn1_down, w_in, conv_w, conv_b, dt_bias, a_log, d_skip,
              ssd_norm_g, sb_norm_g, sb_bias, w_out, ffn2_up, ffn2_down):
    bp = x_prompt.shape[0]
    bs = x_sample.shape[0]
    past_len = page_table.shape[1] * cache_k.shape[2]
    yp, ys = x_prompt, x_sample
    kp_l, vp_l, sp_l, cp_l = [], [], [], []
    ks_l, vs_l, ss_l, cs_l = [], [], [], []
    for l in range(DEPTH):
        params = (norm_g[l], ffn1_up[l], ffn1_down[l], w_in[l], conv_w[l], conv_b[l], dt_bias[l],
                  a_log[l], d_skip[l], ssd_norm_g[l], sb_norm_g[l], sb_bias[l], w_out[l],
                  ffn2_up[l], ffn2_down[l])
        conv0 = jnp.zeros((bp, SSD_CONV_W - 1, CONV_DIM), x_prompt.dtype)
        h0 = jnp.zeros((bp, SSD_HEADS, SSD_HEAD_DIM, SSD_STATE), jnp.float32)
        yp, cp, sp, kp, vp = layer_forward(yp, conv0, h0, None, None, 0, *params)
        past_k = cache_k[l][page_table].reshape(bs, past_len, SB_HEADS, SB_HEAD_DIM)
        past_v = cache_v[l][page_table].reshape(bs, past_len, SB_HEADS, SB_HEAD_DIM)
        ys, cs, ss, kss, vss = layer_forward(ys, state_conv[l], state_ssm[l], past_k, past_v, past_len, *params)
        kp_l.append(kp); vp_l.append(vp); sp_l.append(sp); cp_l.append(cp)
        ks_l.append(kss); vs_l.append(vss); ss_l.append(ss); cs_l.append(cs)
    new_k_prompt = jnp.stack(kp_l)
    new_v_prompt = jnp.stack(vp_l)
    new_ssm_prompt = jnp.stack(sp_l)
    new_conv_prompt = jnp.stack(cp_l)
    new_k_sample = jnp.stack(ks_l)
    new_v_sample = jnp.stack(vs_l)
    new_ssm_sample = jnp.stack(ss_l)
    new_conv_sample = jnp.stack(cs_l)
    return (yp, ys, new_k_prompt, new_v_prompt, new_ssm_prompt, new_conv_prompt,
            new_k_sample, new_v_sample, new_ssm_sample, new_conv_sample)
```

```python
import functools

import jax
import jax.numpy as jnp
from jax import lax
from jax.experimental import pallas as pl
from jax.experimental.pallas import tpu as pltpu

F32 = jnp.float32
BF16 = jnp.bfloat16

RMS_EPS = 1e-6
SSD_HEAD_DIM = 64
SSD_STATE = 64
SSD_GROUPS = 2
SSD_CONV_W = 4
SB_HEAD_DIM = 64
LANES = 128
SUBLANES = 8
TILE = 128
FF_CHUNK = 256
NEG_BIG = -1e30
VMEM_LIMIT = 56 * 1024 * 1024
HIGHEST = lax.Precision.HIGHEST


def _rms(x, g):
    ms = jnp.mean(x * x, axis=-1, keepdims=True)
    return x * lax.rsqrt(ms + RMS_EPS) * g


def _silu(x):
    return x * jax.nn.sigmoid(x)


def _softplus(x):
    return jnp.maximum(x, 0.0) + jnp.log1p(jnp.exp(-jnp.abs(x)))


def _dot(a, b):
    return jnp.dot(a, b, preferred_element_type=F32)


def _ffn_half_step(x, g_pre, g_post, wup_ref, wdn_ref, acc_ref):
    d_ff = wdn_ref.shape[0]
    h = _rms(x, g_pre).astype(BF16)
    for j in range(d_ff // FF_CHUNK):
        lo = j * FF_CHUNK
        gate = _dot(h, wup_ref[:, lo:lo + FF_CHUNK])
        up = _dot(h, wup_ref[:, d_ff + lo:d_ff + lo + FF_CHUNK])
        act = (_silu(gate) * up).astype(BF16)
        part = _dot(act, wdn_ref[lo:lo + FF_CHUNK, :])
        if j == 0:
            acc_ref[...] = part
        else:
            acc_ref[...] += part
    return x + 0.5 * _rms(acc_ref[...], g_post)


def _stage_a_kernel(prompt, n_alias, x_ref, g_ref, wup_ref, wdn_ref, win_ref, *rest):
    rest = rest[n_alias:]
    if prompt:
        (x1_ref, z_ref, xbc_ref, dt_ref, q_ref, ktf_ref, vtf_ref, kt_ref, vb_ref, acc_ref) = rest
    else:
        (x1_ref, z_ref, xbc_ref, dt_ref, q_ref, k_ref, v_ref, acc_ref) = rest
    d_ssd = z_ref.shape[1]
    conv_dim = xbc_ref.shape[1]
    d_sb = q_ref.shape[1]
    x1 = _ffn_half_step(x_ref[...], g_ref[0:1, :], g_ref[1:2, :], wup_ref, wdn_ref, acc_ref)
    x1_ref[...] = x1
    h = _rms(x1, g_ref[2:3, :]).astype(BF16)
    o = 0
    z_ref[...] = _dot(h, win_ref[:, o:o + d_ssd]); o += d_ssd
    xbc_ref[...] = _dot(h, win_ref[:, o:o + conv_dim]); o += conv_dim
    q = _dot(h, win_ref[:, o:o + d_sb]); o += d_sb
    k = _dot(h, win_ref[:, o:o + d_sb]); o += d_sb
    v = _dot(h, win_ref[:, o:o + d_sb]); o += d_sb
    dt_ref[...] = _dot(h, win_ref[:, o:o + LANES])
    q_ref[...] = (q * (SB_HEAD_DIM ** -0.5)).astype(BF16)
    if prompt:
        kt = k.T
        ktf_ref[0, 0] = kt
        kt_ref[...] = kt.astype(BF16)
        vtf_ref[0, 0] = v.T
        vb_ref[...] = v.astype(BF16)
    else:
        k_ref[...] = k
        v_ref[...] = v


def _stage_c_kernel(x_ref, yssd_ref, ysb_ref, g_ref, wout_ref, wup_ref, wdn_ref, o_ref, acc_ref):
    d_ssd = yssd_ref.shape[1]
    m = _dot(yssd_ref[...], wout_ref[0:d_ssd, :]) + _dot(ysb_ref[...], wout_ref[d_ssd:, :])
    x2 = x_ref[...] + _rms(m, g_ref[3:4, :])
    o_ref[...] = _ffn_half_step(x2, g_ref[4:5, :], g_ref[5:6, :], wup_ref, wdn_ref, acc_ref)


def _resident(shape):
    return pl.BlockSpec(shape, lambda *_: (0,) * len(shape), pipeline_mode=pl.Buffered(1))


def _row_block(m):
    return 512 if m % 512 == 0 else m


def _stage_a(x, g, wup, wdn, win, dims):
    m, d = x.shape
    d_ssd, conv_dim, d_sb = dims
    bm = _row_block(m)
    row = lambda w: pl.BlockSpec((bm, w), lambda i: (i, 0))
    out_shape = [
        jax.ShapeDtypeStruct((m, d), F32),
        jax.ShapeDtypeStruct((m, d_ssd), F32),
        jax.ShapeDtypeStruct((m, conv_dim), F32),
        jax.ShapeDtypeStruct((m, LANES), F32),
        jax.ShapeDtypeStruct((m, d_sb), BF16),
        jax.ShapeDtypeStruct((m, d_sb), F32),
        jax.ShapeDtypeStruct((m, d_sb), F32),
    ]
    return pl.pallas_call(
        functools.partial(_stage_a_kernel, False, 0),
        grid=(m // bm,),
        in_specs=[row(d), _resident(g.shape), _resident(wup.shape), _resident(wdn.shape),
                  _resident(win.shape)],
        out_specs=[row(d), row(d_ssd), row(conv_dim), row(LANES), row(d_sb), row(d_sb), row(d_sb)],
        out_shape=out_shape,
        scratch_shapes=[pltpu.VMEM((bm, d), F32)],
        compiler_params=pltpu.CompilerParams(dimension_semantics=("parallel",),
                                             vmem_limit_bytes=VMEM_LIMIT),
        name="stage_a",
    )(x, g, wup, wdn, win)


def _stage_a_prompt(x, g, wup, wdn, win, dims, layer, depth, batch, seq, kv_prev):
    m, d = x.shape
    d_ssd, conv_dim, d_sb = dims
    bm = _row_block(m)
    nb = seq // bm
    row = lambda w: pl.BlockSpec((bm, w), lambda i: (i, 0))
    kv_spec = pl.BlockSpec((1, 1, d_sb, bm), lambda i: (layer, i // nb, 0, i % nb))
    kv_shape = jax.ShapeDtypeStruct((depth, batch, d_sb, seq), F32)
    out_shape = [
        jax.ShapeDtypeStruct((m, d), F32),
        jax.ShapeDtypeStruct((m, d_ssd), F32),
        jax.ShapeDtypeStruct((m, conv_dim), F32),
        jax.ShapeDtypeStruct((m, LANES), F32),
        jax.ShapeDtypeStruct((m, d_sb), BF16),
        kv_shape, kv_shape,
        jax.ShapeDtypeStruct((d_sb, m), BF16),
        jax.ShapeDtypeStruct((m, d_sb), BF16),
    ]
    out_specs = [row(d), row(d_ssd), row(conv_dim), row(LANES), row(d_sb), kv_spec, kv_spec,
                 pl.BlockSpec((d_sb, bm), lambda i: (0, i)), row(d_sb)]
    in_specs = [row(d), _resident(g.shape), _resident(wup.shape), _resident(wdn.shape),
                _resident(win.shape)]
    args = [x, g, wup, wdn, win]
    aliases = {}
    if kv_prev is not None:
        in_specs += [pl.BlockSpec(memory_space=pl.ANY)] * 2
        args += list(kv_prev)
        aliases = {5: 5, 6: 6}
    return pl.pallas_call(
        functools.partial(_stage_a_kernel, True, len(aliases)),
        grid=(m // bm,),
        in_specs=in_specs,
        out_specs=out_specs,
        out_shape=out_shape,
        input_output_aliases=aliases,
        scratch_shapes=[pltpu.VMEM((bm, d), F32)],
        compiler_params=pltpu.CompilerParams(dimension_semantics=("parallel",),
                                             vmem_limit_bytes=VMEM_LIMIT),
        name="stage_a_prompt",
    )(*args)


def _stage_c(x1, yssd, ysb, g, wout, wup, wdn):
    m, d = x1.shape
    bm = _row_block(m)
    row = lambda w: pl.BlockSpec((bm, w), lambda i: (i, 0))
    return pl.pallas_call(
        _stage_c_kernel,
        grid=(m // bm,),
        in_specs=[row(d), row(yssd.shape[1]), row(ysb.shape[1]), _resident(g.shape),
                  _resident(wout.shape), _resident(wup.shape), _resident(wdn.shape)],
        out_specs=row(d),
        out_shape=jax.ShapeDtypeStruct((m, d), F32),
        scratch_shapes=[pltpu.VMEM((bm, d), F32)],
        compiler_params=pltpu.CompilerParams(dimension_semantics=("parallel",),
                                             vmem_limit_bytes=VMEM_LIMIT),
        name="stage_c",
    )(x1, yssd, ysb, g, wout, wup, wdn)


def _ssd_prompt_kernel(xbc_ref, z_ref, dt_ref, convw_ref, convb_ref, dtb_ref, alog_ref, dskip_ref,
                       g_ref, y_ref, convout_ref, hfin_ref, cbuf, state, ybuf):
    c = pl.program_id(1)
    d_ssd = z_ref.shape[1]
    n_heads = d_ssd // SSD_HEAD_DIM
    heads_per_group = n_heads // SSD_GROUPS
    gn = SSD_GROUPS * SSD_STATE

    @pl.when(c == 0)
    def _():
        cbuf[0:SUBLANES, :] = jnp.zeros((SUBLANES, cbuf.shape[1]), F32)
        state[...] = jnp.zeros(state.shape, F32)

    x_raw = xbc_ref[...]
    cbuf[SUBLANES:SUBLANES + TILE, :] = x_raw
    w = convw_ref[...]
    s = w[0:1, :] * cbuf[SUBLANES - 3:SUBLANES - 3 + TILE, :]
    s = s + w[1:2, :] * cbuf[SUBLANES - 2:SUBLANES - 2 + TILE, :]
    s = s + w[2:3, :] * cbuf[SUBLANES - 1:SUBLANES - 1 + TILE, :]
    s = s + w[3:4, :] * x_raw
    xc = _silu(convb_ref[...] + s)
    cbuf[0:SUBLANES, :] = x_raw[TILE - SUBLANES:, :]

    xs = xc[:, :d_ssd]
    bm = xc[:, d_ssd:d_ssd + gn]
    cm = xc[:, d_ssd + gn:]

    dt = _softplus(dt_ref[...] + dtb_ref[...])
    da = dt * (-jnp.exp(alog_ref[...]))
    r_i = lax.broadcasted_iota(jnp.int32, (TILE, TILE), 0)
    c_i = lax.broadcasted_iota(jnp.int32, (TILE, TILE), 1)
    causal = c_i <= r_i
    tri = causal.astype(F32)
    acum = jnp.dot(tri, da, precision=HIGHEST, preferred_element_type=F32)
    acum_t = acum.T
    dt_t = dt.T
    a_last = acum[TILE - 1:TILE, :]
    w_end = jnp.exp(a_last - acum) * dt
    e_acum = jnp.exp(acum)
    e_last = jnp.exp(a_last)

    for g in range(SSD_GROUPS):
        b_g = bm[:, g * SSD_STATE:(g + 1) * SSD_STATE]
        c_g = cm[:, g * SSD_STATE:(g + 1) * SSD_STATE]
        cb = lax.dot_general(c_g, b_g, (((1,), (1,)), ((), ())), precision=HIGHEST,
                             preferred_element_type=F32)
        for r in range(heads_per_group):
            h = g * heads_per_group + r
            x_h = xs[:, h * SSD_HEAD_DIM:(h + 1) * SSD_HEAD_DIM]
            seg = acum[:, h:h + 1] - acum_t[h:h + 1, :]
            decay = jnp.exp(jnp.where(causal, seg, NEG_BIG))
            wmat = cb * decay * dt_t[h:h + 1, :]
            y_diag = jnp.dot(wmat, x_h, precision=HIGHEST, preferred_element_type=F32)
            st = state[h]
            y_off = lax.dot_general(c_g, st, (((1,), (1,)), ((), ())), precision=HIGHEST,
                                    preferred_element_type=F32) * e_acum[:, h:h + 1]
            ybuf[:, h * SSD_HEAD_DIM:(h + 1) * SSD_HEAD_DIM] = y_diag + y_off
            upd = lax.dot_general(x_h * w_end[:, h:h + 1], b_g, (((0,), (0,)), ((), ())),
                                  precision=HIGHEST, preferred_element_type=F32)
            state[h] = e_last[:, h:h + 1] * st + upd

    y = ybuf[...] + dskip_ref[...] * xs
    y_ref[...] = _rms(y * _silu(z_ref[...]), g_ref[...]).astype(y_ref.dtype)

    @pl.when(c == pl.num_programs(1) - 1)
    def _():
        convout_ref[0] = x_raw[TILE - (SSD_CONV_W - 1):, :]
        hfin_ref[0] = state[...]


def _ssd_prompt(xbc, z, dt, convw, convb, dtb, alog, dskip_full, g, batch, seq):
    m, conv_dim = xbc.shape
    d_ssd = z.shape[1]
    n_heads = d_ssd // SSD_HEAD_DIM
    nc = seq // TILE
    row = lambda w: pl.BlockSpec((TILE, w), lambda b, c: (b * nc + c, 0))
    const = lambda a: pl.BlockSpec(a.shape, lambda b, c: (0,) * a.ndim)
    return pl.pallas_call(
        _ssd_prompt_kernel,
        grid=(batch, nc),
        in_specs=[row(conv_dim), row(d_ssd), row(LANES), const(convw), const(convb), const(dtb),
                  const(alog), const(dskip_full), const(g)],
        out_specs=[row(d_ssd),
                   pl.BlockSpec((1, SSD_CONV_W - 1, conv_dim), lambda b, c: (b, 0, 0)),
                   pl.BlockSpec((1, n_heads, SSD_HEAD_DIM, SSD_STATE), lambda b, c: (b, 0, 0, 0))],
        out_shape=[jax.ShapeDtypeStruct((m, d_ssd), BF16),
                   jax.ShapeDtypeStruct((batch, SSD_CONV_W - 1, conv_dim), F32),
                   jax.ShapeDtypeStruct((batch, n_heads, SSD_HEAD_DIM, SSD_STATE), F32)],
        scratch_shapes=[pltpu.VMEM((SUBLANES + TILE, conv_dim), F32),
                        pltpu.VMEM((n_heads, SSD_HEAD_DIM, SSD_STATE), F32),
                        pltpu.VMEM((TILE, d_ssd), F32)],
        compiler_params=pltpu.CompilerParams(dimension_semantics=("parallel", "arbitrary"),
                                             vmem_limit_bytes=VMEM_LIMIT),
        name="ssd_prompt",
    )(xbc, z, dt, convw, convb, dtb, alog, dskip_full, g)


def _ssd_step_kernel(xbc_ref, z_ref, dt_ref, conv_ref, st_ref, convw_ref, convb_ref, dtb_ref,
                     alog_ref, dskip_ref, g_ref, y_ref, convout_ref, stout_ref):
    d_ssd = z_ref.shape[2]
    n_heads = d_ssd // SSD_HEAD_DIM
    heads_per_group = n_heads // SSD_GROUPS
    gn = SSD_GROUPS * SSD_STATE
    x_raw = xbc_ref[0]
    prev = conv_ref[0]
    w = convw_ref[...]
    s = w[0:1, :] * prev[0:1, :]
    s = s + w[1:2, :] * prev[1:2, :]
    s = s + w[2:3, :] * prev[2:3, :]
    s = s + w[3:4, :] * x_raw
    xc = _silu(convb_ref[...] + s)
    convout_ref[0, 0:SSD_CONV_W - 2, :] = prev[1:, :]
    convout_ref[0, SSD_CONV_W - 2:, :] = x_raw

    xs = xc[:, :d_ssd]
    dt = _softplus(dt_ref[0] + dtb_ref[...])
    decay = jnp.exp(dt * (-jnp.exp(alog_ref[...])))
    eye = (lax.broadcasted_iota(jnp.int32, (SSD_HEAD_DIM, SSD_HEAD_DIM), 0)
           == lax.broadcasted_iota(jnp.int32, (SSD_HEAD_DIM, SSD_HEAD_DIM), 1)).astype(F32)
    y_parts = []
    for h in range(n_heads):
        g = h // heads_per_group
        x_h = xs[:, h * SSD_HEAD_DIM:(h + 1) * SSD_HEAD_DIM]
        b_g = xc[:, d_ssd + g * SSD_STATE:d_ssd + (g + 1) * SSD_STATE]
        c_g = xc[:, d_ssd + gn + g * SSD_STATE:d_ssd + gn + (g + 1) * SSD_STATE]
        x_col = jnp.sum(eye * (x_h * dt[:, h:h + 1]), axis=1, keepdims=True)
        st = decay[:, h:h + 1] * st_ref[0, h] + x_col * b_g
        stout_ref[0, h] = st
        y_col = jnp.sum(st * c_g, axis=1, keepdims=True)
        y_parts.append(jnp.sum(eye * y_col, axis=0, keepdims=True))
    y = jnp.concatenate(y_parts, axis=1) + dskip_ref[...] * xs
    y_ref[0] = _rms(y * _silu(z_ref[0]), g_ref[...]).astype(y_ref.dtype)


def _ssd_step(xbc, z, dt, conv_state, ssm_state, convw, convb, dtb, alog, dskip_full, g):
    n, conv_dim = xbc.shape
    d_ssd = z.shape[1]
    n_heads = d_ssd // SSD_HEAD_DIM
    per_seq = lambda *tail: pl.BlockSpec((1,) + tail, lambda b: (b,) + (0,) * len(tail))
    const = lambda a: pl.BlockSpec(a.shape, lambda b: (0,) * a.ndim)
    y, conv_new, st_new = pl.pallas_call(
        _ssd_step_kernel,
        grid=(n,),
        in_specs=[per_seq(1, conv_dim), per_seq(1, d_ssd), per_seq(1, LANES),
                  per_seq(SSD_CONV_W - 1, conv_dim), per_seq(n_heads, SSD_HEAD_DIM, SSD_STATE),
                  const(convw), const(convb), const(dtb), const(alog), const(dskip_full), const(g)],
        out_specs=[per_seq(1, d_ssd), per_seq(SSD_CONV_W - 1, conv_dim),
                   per_seq(n_heads, SSD_HEAD_DIM, SSD_STATE)],
        out_shape=[jax.ShapeDtypeStruct((n, 1, d_ssd), BF16),
                   jax.ShapeDtypeStruct((n, SSD_CONV_W - 1, conv_dim), F32),
                   jax.ShapeDtypeStruct((n, n_heads, SSD_HEAD_DIM, SSD_STATE), F32)],
        compiler_params=pltpu.CompilerParams(dimension_semantics=("parallel",)),
        name="ssd_step",
    )(xbc.reshape(n, 1, conv_dim), z.reshape(n, 1, d_ssd), dt.reshape(n, 1, LANES), conv_state,
      ssm_state, convw, convb, dtb, alog, dskip_full, g)
    return y.reshape(n, d_ssd), conv_new, st_new


def _log_terms(z):
    lk = -_softplus(z)
    return lk, z + lk


def _suffix_sums(lk, cum_rhs):
    hi = lk.astype(BF16)
    lo = (lk - hi.astype(F32)).astype(BF16)
    r = _dot(jnp.concatenate([hi, lo], axis=1), cum_rhs)
    return r[:, :TILE], r[:, TILE:]


def _cum_rhs():
    j = lax.broadcasted_iota(jnp.int32, (TILE, TILE), 0)
    s = lax.broadcasted_iota(jnp.int32, (TILE, TILE), 1)
    half = jnp.concatenate([(j > s).astype(BF16), jnp.ones((TILE, TILE), BF16)], axis=1)
    return jnp.concatenate([half, half], axis=0)


def _sb_prompt_kernel(bias_ref, q_ref, kt_ref, v_ref, g_ref, cum_ref, o_ref, carry, acc, obuf):
    qi = pl.program_id(1)
    d_sb = q_ref.shape[1]
    n_heads = d_sb // SB_HEAD_DIM
    q = q_ref[...]
    cum_rhs = cum_ref[...]
    r_i = lax.broadcasted_iota(jnp.int32, (TILE, TILE), 0)
    c_i = lax.broadcasted_iota(jnp.int32, (TILE, TILE), 1)
    strict = c_i < r_i

    def tile(kb, h, first):
        k0 = pl.multiple_of(kb * TILE, TILE)
        q_h = q[:, h * SB_HEAD_DIM:(h + 1) * SB_HEAD_DIM]
        kt = kt_ref[h * SB_HEAD_DIM:(h + 1) * SB_HEAD_DIM, pl.ds(k0, TILE)]
        z = _dot(q_h, kt) + bias_ref[h]
        lk, lb = _log_terms(z)
        if first:
            lk = jnp.where(strict, lk, 0.0)
        after, tot = _suffix_sums(lk, cum_rhs)
        if first:
            p = jnp.where(strict, jnp.exp(lb + after), 0.0)
        else:
            p = jnp.exp(lb + after + carry[h])
        pair = (h // 2) * LANES
        pv = _dot(p.astype(BF16), v_ref[pl.ds(k0, TILE), pair:pair + LANES])
        if first:
            carry[h] = tot
            acc[h] = pv
        else:
            carry[h] += tot
            acc[h] += pv

    for h in range(n_heads):
        tile(qi, h, True)

    def body(i, _):
        for h in range(n_heads):
            tile(qi - 1 - i, h, False)
        return 0

    lax.fori_loop(0, qi, body, 0)

    for h in range(n_heads):
        off = (h % 2) * SB_HEAD_DIM
        obuf[:, h * SB_HEAD_DIM:(h + 1) * SB_HEAD_DIM] = acc[h][:, off:off + SB_HEAD_DIM]
    o_ref[...] = _rms(obuf[...], g_ref[...]).astype(o_ref.dtype)


def _sb_prompt(bias, q, kt, vb, g, batch, seq):
    m, d_sb = q.shape
    n_heads = d_sb // SB_HEAD_DIM
    nq = seq // TILE
    cum = _cum_rhs()
    return pl.pallas_call(
        _sb_prompt_kernel,
        grid_spec=pltpu.PrefetchScalarGridSpec(
            num_scalar_prefetch=0,
            grid=(batch, nq),
            in_specs=[pl.BlockSpec(memory_space=pltpu.SMEM),
                      pl.BlockSpec((TILE, d_sb), lambda b, i: (b * nq + i, 0)),
                      pl.BlockSpec((d_sb, seq), lambda b, i: (0, b)),
                      pl.BlockSpec((seq, d_sb), lambda b, i: (b, 0)),
                      pl.BlockSpec(g.shape, lambda b, i: (0, 0)),
                      pl.BlockSpec(cum.shape, lambda b, i: (0, 0))],
            out_specs=pl.BlockSpec((TILE, d_sb), lambda b, i: (b * nq + i, 0)),
            scratch_shapes=[pltpu.VMEM((n_heads, TILE, TILE), F32),
                            pltpu.VMEM((n_heads, TILE, LANES), F32),
                            pltpu.VMEM((TILE, d_sb), F32)]),
        out_shape=jax.ShapeDtypeStruct((m, d_sb), BF16),
        compiler_params=pltpu.CompilerParams(dimension_semantics=("parallel", "arbitrary"),
                                             vmem_limit_bytes=VMEM_LIMIT),
        name="sb_prompt",
    )(bias, q, kt, vb, g, cum)


PAGES_PER_STEP = 8


def _sb_paged_kernel(pt_ref, q_ref, bias_ref, g_ref, cum_ref, *rest):
    k_refs = rest[:PAGES_PER_STEP]
    v_refs = rest[PAGES_PER_STEP:2 * PAGES_PER_STEP]
    o_ref, qbd, carry, acc = rest[2 * PAGES_PER_STEP:]
    j = pl.program_id(1)
    d_sb = q_ref.shape[2]
    n_heads = d_sb // SB_HEAD_DIM
    head_of_col = lax.broadcasted_iota(jnp.int32, (n_heads, d_sb), 1) // SB_HEAD_DIM
    own = head_of_col == lax.broadcasted_iota(jnp.int32, (n_heads, d_sb), 0)

    @pl.when(j == 0)
    def _():
        qbd[...] = jnp.where(own, q_ref[0].astype(F32), 0.0).astype(BF16)
        carry[...] = jnp.zeros(carry.shape, F32)
        acc[...] = jnp.zeros(acc.shape, F32)

    cum_rhs = cum_ref[...]
    bias = bias_ref[...]
    for i in range(PAGES_PER_STEP):
        kt = k_refs[i][0, 0].astype(BF16)
        vt = v_refs[i][0, 0].astype(BF16)
        z = _dot(qbd[...], kt) + bias
        lk, lb = _log_terms(z)
        after, tot = _suffix_sums(lk, cum_rhs)
        p = jnp.exp(lb + after + carry[...])
        acc[...] += lax.dot_general(p.astype(BF16), vt, (((1,), (1,)), ((), ())),
                                    preferred_element_type=F32)
        carry[...] += tot

    @pl.when(j == pl.num_programs(1) - 1)
    def _():
        o = jnp.sum(jnp.where(own, acc[...], 0.0), axis=0, keepdims=True)
        o_ref[0] = _rms(o, g_ref[...]).astype(o_ref.dtype)


def _sb_paged(page_table, q, bias_tile, g, cache_k, cache_v, layer):
    n, d_sb = q.shape
    n_heads = d_sb // SB_HEAD_DIM
    depth, n_pool, page, _, _ = cache_k.shape
    assert page == TILE
    n_pages = page_table.shape[1]
    assert n_pages % PAGES_PER_STEP == 0
    ck = jnp.transpose(cache_k, (0, 1, 3, 4, 2)).reshape(depth, n_pool, d_sb, page)
    cv = jnp.transpose(cache_v, (0, 1, 3, 4, 2)).reshape(depth, n_pool, d_sb, page)
    cum = _cum_rhs()

    def page_spec(i):
        def index(b, j, pt):
            return (layer, pt[b, n_pages - 1 - (j * PAGES_PER_STEP + i)], 0, 0)
        return pl.BlockSpec((1, 1, d_sb, page), index)

    const = lambda a: pl.BlockSpec(a.shape, lambda b, j, pt: (0,) * a.ndim)
    y = pl.pallas_call(
        _sb_paged_kernel,
        grid_spec=pltpu.PrefetchScalarGridSpec(
            num_scalar_prefetch=1,
            grid=(n, n_pages // PAGES_PER_STEP),
            in_specs=[pl.BlockSpec((1, 1, d_sb), lambda b, j, pt: (b, 0, 0)), const(bias_tile),
                      const(g), const(cum)]
                     + [page_spec(i) for i in range(PAGES_PER_STEP)] * 2,
            out_specs=pl.BlockSpec((1, 1, d_sb), lambda b, j, pt: (b, 0, 0)),
            scratch_shapes=[pltpu.VMEM((n_heads, d_sb), BF16),
                            pltpu.VMEM((n_heads, TILE), F32),
                            pltpu.VMEM((n_heads, d_sb), F32)]),
        out_shape=jax.ShapeDtypeStruct((n, 1, d_sb), BF16),
        compiler_params=pltpu.CompilerParams(dimension_semantics=("parallel", "arbitrary"),
                                             vmem_limit_bytes=VMEM_LIMIT),
        name="sb_paged",
    )(page_table, q.reshape(n, 1, d_sb), bias_tile, g, cum,
      *([ck] * PAGES_PER_STEP), *([cv] * PAGES_PER_STEP))
    return y.reshape(n, d_sb)


def _lane_pad(v):
    return jnp.zeros((1, LANES), F32).at[0, :v.shape[0]].set(v.astype(F32))


def kernel(x_prompt, x_sample, cache_k, cache_v, state_ssm, state_conv, page_table, norm_g, ffn1_up,
           ffn1_down, w_in, conv_w, conv_b, dt_bias, a_log, d_skip, ssd_norm_g, sb_norm_g, sb_bias,
           w_out, ffn2_up, ffn2_down):
    batch, seq, d_model = x_prompt.shape
    n_dec = x_sample.shape[0]
    depth = norm_g.shape[0]
    d_ssd = ssd_norm_g.shape[1]
    d_sb = sb_norm_g.shape[1]
    conv_dim = conv_w.shape[2]
    n_ssd_heads = d_ssd // SSD_HEAD_DIM
    n_sb_heads = d_sb // SB_HEAD_DIM
    dims = (d_ssd, conv_dim, d_sb)

    o1 = d_ssd
    o2 = o1 + conv_dim
    o3 = o2 + n_ssd_heads
    w_dt = jnp.zeros((depth, d_model, LANES), F32).at[:, :, :n_ssd_heads].set(w_in[:, :, o2:o3])
    win = jnp.concatenate([w_in[:, :, :o2], w_in[:, :, o3:], w_dt], axis=2).astype(BF16)
    wup1, wdn1 = ffn1_up.astype(BF16), ffn1_down.astype(BF16)
    wup2, wdn2 = ffn2_up.astype(BF16), ffn2_down.astype(BF16)
    wout = w_out.astype(BF16)

    xp = x_prompt.reshape(batch * seq, d_model)
    xs = x_sample.reshape(n_dec, d_model)
    outs = {name: [] for name in ("sp", "cp", "ks", "vs", "ss", "cs")}
    kv_prompt = None
    for l in range(depth):
        g = norm_g[l]
        convb = conv_b[l].reshape(1, conv_dim)
        dtb, alog = _lane_pad(dt_bias[l]), _lane_pad(a_log[l])
        dskip_full = jnp.repeat(d_skip[l], SSD_HEAD_DIM).reshape(1, d_ssd)
        g_ssd = ssd_norm_g[l].reshape(1, d_ssd)
        g_sb = sb_norm_g[l].reshape(1, d_sb)
        bias_tile = jnp.broadcast_to(sb_bias[l].reshape(n_sb_heads, 1), (n_sb_heads, TILE))

        x1, z, xbc, dt, q, ktf, vtf, kt, vb = _stage_a_prompt(xp, g, wup1[l], wdn1[l], win[l], dims, l,
                                                              depth, batch, seq, kv_prompt)
        kv_prompt = (ktf, vtf)
        yssd, conv_p, st_p = _ssd_prompt(xbc, z, dt, conv_w[l], convb, dtb, alog, dskip_full, g_ssd,
                                         batch, seq)
        ysb = _sb_prompt(sb_bias[l], q, kt, vb, g_sb, batch, seq)
        xp = _stage_c(x1, yssd, ysb, g, wout[l], wup2[l], wdn2[l])
        outs["sp"].append(st_p)
        outs["cp"].append(conv_p)

        x1, z, xbc, dt, q, k, v = _stage_a(xs, g, wup1[l], wdn1[l], win[l], dims)
        yssd, conv_s, st_s = _ssd_step(xbc, z, dt, state_conv[l], state_ssm[l], conv_w[l], convb, dtb,
                                       alog, dskip_full, g_ssd)
        ysb = _sb_paged(page_table, q, bias_tile, g_sb, cache_k, cache_v, l)
        xs = _stage_c(x1, yssd, ysb, g, wout[l], wup2[l], wdn2[l])
        outs["ks"].append(k.reshape(n_dec, 1, n_sb_heads, SB_HEAD_DIM))
        outs["vs"].append(v.reshape(n_dec, 1, n_sb_heads, SB_HEAD_DIM))
        outs["ss"].append(st_s)
        outs["cs"].append(conv_s)

    stack = lambda name: jnp.stack(outs[name])
    token_major = lambda t: jnp.transpose(
        t.reshape(depth, batch, n_sb_heads, SB_HEAD_DIM, seq), (0, 1, 4, 2, 3))
    return (xp.reshape(batch, seq, d_model), xs.reshape(n_dec, 1, d_model),
            token_major(kv_prompt[0]), token_major(kv_prompt[1]), stack("sp"), stack("cp"),
            stack("ks"), stack("vs"), stack("ss"), stack("cs"))
```
